```python
import jax, jax.numpy as jnp
from jax import lax
import numpy as np

D_MODEL = 1024
BATCH = 16
SEQ = 2048
DEPTH = 2

N_A_LAYERS = DEPTH // 2
N_B_LAYERS = DEPTH - N_A_LAYERS
HGRN_HEAD_DIM = 128
HGRN_HEADS = D_MODEL // HGRN_HEAD_DIM
HGRN_DK = HGRN_HEAD_DIM
HGRN_DV = HGRN_HEAD_DIM
HGRN_CHUNK = 64
ATTN_HEAD_DIM = 64
ATTN_HEADS = D_MODEL // ATTN_HEAD_DIM
ATTN_KV_HEADS = ATTN_HEADS // 4
ATTN_GROUP = ATTN_HEADS // ATTN_KV_HEADS
WINDOW = 128
ATTN_BLOCK = 128
N_EXPERTS = 32
TOP_K = 4
D_FF_EXPERT = D_MODEL
SWIGLU_LIMIT = 7.0
SWIGLU_ALPHA = 1.702
MOE_BLOCK = 128
NORM_EPS = 1e-5

kernel_name = 'hybrid_hgrn2_swa_sink_moe'

F32 = jnp.float32


def rmsnorm(x, w):
    x32 = x.astype(F32)
    y = x32 * lax.rsqrt(jnp.mean(jnp.square(x32), axis=-1, keepdims=True) + NORM_EPS)
    return (y * w.astype(F32)).astype(x.dtype)


def modulate(xn, shift, scale):
    return xn * (1.0 + scale[:, None, :]) + shift[:, None, :]


def hgrn2_mixer(h, w_in, lb, norm_w, w_out):
    b, s, d = h.shape
    nc = s // HGRN_CHUNK
    q, f_logit, i_in, g_out = jnp.split(h @ w_in, 4, axis=-1)
    f = lb + (1.0 - lb) * jax.nn.sigmoid(f_logit.astype(F32))
    log_f = jnp.log(f)
    k = 1.0 - f
    q = jax.nn.silu(q.astype(F32))
    v = i_in.astype(F32)

    def to_chunks(t):
        return t.reshape(b, nc, HGRN_CHUNK, HGRN_HEADS, -1).transpose(1, 0, 3, 2, 4)

    causal = jnp.tril(jnp.ones((HGRN_CHUNK, HGRN_CHUNK), dtype=bool))

    def chunk_step(state, inp):
        qc, kc, vc, gc = inp
        G = jnp.cumsum(gc, axis=2)
        diff = G[:, :, :, None, :] - G[:, :, None, :, :]
        decay = jnp.exp(jnp.where(causal[:, :, None], diff, -jnp.inf))
        scores = jnp.einsum('bhtd,bhsd,bhtsd->bhts', qc, kc, decay)
        o = (jnp.einsum('bhts,bhsv->bhtv', scores, vc)
             + jnp.einsum('bhtd,bhdv->bhtv', qc * jnp.exp(G), state))
        G_end = G[:, :, -1, :]
        state = (jnp.exp(G_end)[..., None] * state
                 + jnp.einsum('bhsd,bhsv->bhdv', kc * jnp.exp(G_end[:, :, None, :] - G), vc))
        return state, o

    state0 = jnp.zeros((b, HGRN_HEADS, HGRN_DK, HGRN_DV), F32)
    _, o = lax.scan(chunk_step, state0, (to_chunks(q), to_chunks(k), to_chunks(v), to_chunks(log_f)))
    o = o.transpose(1, 0, 3, 2, 4).reshape(b, s, HGRN_HEADS, HGRN_DV)
    o = rmsnorm(o, norm_w).reshape(b, s, d).astype(h.dtype)
    return (o * jax.nn.silu(g_out)) @ w_out


def shared_kv_bands(h_kv, w_kv):
    b, s, _ = h_kv.shape
    nb = s // ATTN_BLOCK
    k, v = jnp.split(h_kv @ w_kv, 2, axis=-1)

    def band(t):
        t = t.reshape(b, nb, ATTN_BLOCK, ATTN_KV_HEADS, ATTN_HEAD_DIM)
        prev = jnp.pad(t, ((0, 0), (1, 0), (0, 0), (0, 0), (0, 0)))[:, :-1]
        return jnp.concatenate([prev, t], axis=2)

    return band(k), band(v)


def alibi_band_bias(nb):
    qi = jnp.arange(ATTN_BLOCK)[:, None]
    ki = jnp.arange(2 * ATTN_BLOCK)[None, :]
    dist = ATTN_BLOCK + qi - ki
    valid = (dist >= 0) & (dist < WINDOW)
    blk = jnp.arange(nb)[:, None, None]
    valid = valid[None] & ((blk > 0) | (ki[None] >= ATTN_BLOCK))
    slopes = jnp.exp2(-8.0 * jnp.arange(1, ATTN_HEADS + 1, dtype=F32) / ATTN_HEADS)
    slopes = slopes.reshape(ATTN_KV_HEADS, ATTN_GROUP)
    bias = -slopes[:, :, None, None] * dist.astype(F32)[None, None]
    return valid, bias


def swa_sink_mixer(h, w_q, k_band, v_band, sinks, w_out, valid, bias):
    b, s, _ = h.shape
    nb = s // ATTN_BLOCK
    q = (h @ w_q).reshape(b, nb, ATTN_BLOCK, ATTN_KV_HEADS, ATTN_GROUP, ATTN_HEAD_DIM)
    scores = jnp.einsum('bnqhgd,bnkhd->bhgnqk', q, k_band, preferred_element_type=F32)
    scores = scores * (ATTN_HEAD_DIM ** -0.5) + bias[:, :, None]
    scores = jnp.where(valid, scores, -jnp.inf)
    sink = jnp.broadcast_to(sinks.astype(F32).reshape(ATTN_KV_HEADS, ATTN_GROUP, 1, 1, 1),
                            scores.shape[:-1] + (1,))
    probs = jax.nn.softmax(jnp.concatenate([scores, sink], axis=-1), axis=-1)[..., :-1]
    o = jnp.einsum('bhgnqk,bnkhd->bnqhgd', probs.astype(v_band.dtype), v_band)
    return o.reshape(b, s, ATTN_HEADS * ATTN_HEAD_DIM) @ w_out


def clamped_swiglu(u):
    glu, lin = jnp.split(u, 2, axis=-1)
    glu = jnp.minimum(glu, SWIGLU_LIMIT)
    lin = jnp.clip(lin, -SWIGLU_LIMIT, SWIGLU_LIMIT)
    return glu * jax.nn.sigmoid(SWIGLU_ALPHA * glu) * (lin + 1.0)


def moe_ffn(h, w_r, b_r, w1, b1, w2, b2):
    b, s, d = h.shape
    xf = h.reshape(-1, d)
    n_tok = xf.shape[0]
    n_assign = n_tok * TOP_K
    logits = xf @ w_r + b_r
    top_vals, top_idx = lax.top_k(logits, TOP_K)
    gates = jax.nn.softmax(top_vals.astype(F32), axis=-1).astype(h.dtype)
    e_flat = top_idx.reshape(-1)
    tok_flat = jnp.arange(n_assign) // TOP_K
    g_flat = gates.reshape(-1)
    order = jnp.argsort(e_flat)
    e_s, tok_s, g_s = e_flat[order], tok_flat[order], g_flat[order]
    counts = jnp.bincount(e_flat, length=N_EXPERTS)
    start = jnp.cumsum(counts) - counts
    padded = (counts + MOE_BLOCK - 1) // MOE_BLOCK * MOE_BLOCK
    pend = jnp.cumsum(padded)
    pstart = pend - padded
    dest = pstart[e_s] + (jnp.arange(n_assign) - start[e_s])
    n_blocks = -(-n_assign // MOE_BLOCK) + N_EXPERTS
    n_rows = n_blocks * MOE_BLOCK
    row_tok = jnp.zeros((n_rows,), jnp.int32).at[dest].set(tok_s.astype(jnp.int32))
    row_gate = jnp.zeros((n_rows,), h.dtype).at[dest].set(g_s)
    blk_exp = jnp.minimum(jnp.searchsorted(pend, jnp.arange(n_blocks) * MOE_BLOCK, side='right'),
                          N_EXPERTS - 1)
    xb = xf[row_tok].reshape(n_blocks, MOE_BLOCK, d)

    def expert_block(args):
        xblk, e = args
        u = xblk @ w1[e] + b1[e]
        return clamped_swiglu(u) @ w2[e] + b2[e]

    yb = lax.map(expert_block, (xb, blk_exp))
    y = yb.reshape(n_rows, d) * row_gate[:, None]
    out = jnp.zeros_like(xf).at[row_tok].add(y)
    return out.reshape(b, s, d)


def setup_inputs(seed: int = 0) -> dict:
    key = jax.random.key(seed)
    ks = jax.random.split(key, 24)
    d = D_MODEL
    hq = ATTN_HEADS * ATTN_HEAD_DIM

    def nrm(k, shape, scale):
        return jax.random.normal(k, shape, F32) * scale

    return {
        'x': nrm(ks[0], (BATCH, SEQ, d), 1.0),
        'c': nrm(ks[1], (BATCH, d), 1.0),
        'ada_w': nrm(ks[2], (DEPTH, d, 6 * d), 0.5 * d ** -0.5),
        'ada_b': nrm(ks[3], (DEPTH, 6 * d), 0.02),
        'mix_norm_w': 1.0 + nrm(ks[4], (DEPTH, d), 0.02),
        'ffn_norm_w': 1.0 + nrm(ks[5], (DEPTH, d), 0.02),
        'hgrn_w_in': nrm(ks[6], (N_A_LAYERS, d, 4 * d), d ** -0.5),
        'hgrn_lb_logits': nrm(ks[7], (N_A_LAYERS + 1, d), 0.1),
        'hgrn_norm_w': 1.0 + nrm(ks[8], (N_A_LAYERS, HGRN_DV), 0.02),
        'hgrn_w_out': nrm(ks[9], (N_A_LAYERS, d, d), d ** -0.5),
        'kv_norm_w': 1.0 + nrm(ks[10], (d,), 0.02),
        'kv_ada_w': nrm(ks[11], (d, 2 * d), 0.5 * d ** -0.5),
        'kv_ada_b': nrm(ks[12], (2 * d,), 0.02),
        'w_kv': nrm(ks[13], (d, 2 * ATTN_KV_HEADS * ATTN_HEAD_DIM), d ** -0.5),
        'attn_w_q': nrm(ks[14], (N_B_LAYERS, d, hq), d ** -0.5),
        'attn_sinks': nrm(ks[15], (N_B_LAYERS, ATTN_HEADS), 0.5),
        'attn_w_out': nrm(ks[16], (N_B_LAYERS, hq, d), hq ** -0.5),
        'router_w': nrm(ks[17], (DEPTH, d, N_EXPERTS), d ** -0.5),
        'router_b': nrm(ks[18], (DEPTH, N_EXPERTS), 0.01),
        'moe_w1': nrm(ks[19], (DEPTH, N_EXPERTS, d, 2 * D_FF_EXPERT), d ** -0.5),
        'moe_b1': nrm(ks[20], (DEPTH, N_EXPERTS, 2 * D_FF_EXPERT), 0.02),
        'moe_w2': nrm(ks[21], (DEPTH, N_EXPERTS, D_FF_EXPERT, d), D_FF_EXPERT ** -0.5),
        'moe_b2': nrm(ks[22], (DEPTH, N_EXPERTS, d), 0.02),
        'final_norm_w': 1.0 + nrm(ks[23], (d,), 0.02),
    }


def reference(x, c, ada_w, ada_b, mix_norm_w, ffn_norm_w, hgrn_w_in, hgrn_lb_logits, hgrn_norm_w,
              hgrn_w_out, kv_norm_w, kv_ada_w, kv_ada_b, w_kv, attn_w_q, attn_sinks, attn_w_out,
              router_w, router_b, moe_w1, moe_b1, moe_w2, moe_b2, final_norm_w):
    s = x.shape[1]
    c_act = jax.nn.silu(c)
    lower_bounds = jnp.cumsum(jax.nn.softmax(hgrn_lb_logits.astype(F32), axis=0), axis=0)[:N_A_LAYERS]
    valid, bias = alibi_band_bias(s // ATTN_BLOCK)
    k_band, v_band = None, None
    for layer in range(DEPTH):
        if layer == N_A_LAYERS:
            kv_shift, kv_scale = jnp.split(c_act @ kv_ada_w + kv_ada_b, 2, axis=-1)
            h_kv = modulate(rmsnorm(x, kv_norm_w), kv_shift, kv_scale)
            k_band, v_band = shared_kv_bands(h_kv, w_kv)
        sh_m, sc_m, gt_m, sh_f, sc_f, gt_f = jnp.split(c_act @ ada_w[layer] + ada_b[layer], 6, axis=-1)
        h = modulate(rmsnorm(x, mix_norm_w[layer]), sh_m, sc_m)
        if layer < N_A_LAYERS:
            y = hgrn2_mixer(h, hgrn_w_in[layer], lower_bounds[layer], hgrn_norm_w[layer], hgrn_w_out[layer])
        else:
            j = layer - N_A_LAYERS
            y = swa_sink_mixer(h, attn_w_q[j], k_band, v_band, attn_sinks[j], attn_w_out[j], valid, bias)
        x = x + gt_m[:, None, :] * y
        h = modulate(rmsnorm(x, ffn_norm_w[layer]), sh_f, sc_f)
        y = moe_ffn(h, router_w[layer], router_b[layer], moe_w1[layer], moe_b1[layer],
                    moe_w2[layer], moe_b2[layer])
        x = x + gt_f[:, None, :] * y
    return rmsnorm(x, final_norm_w)
```

```python
import functools

import jax
import jax.numpy as jnp
from jax import lax
from jax.experimental import pallas as pl
from jax.experimental.pallas import tpu as pltpu

F32 = jnp.float32
BF16 = jnp.bfloat16
I32 = jnp.int32
U32 = jnp.uint32

HGRN_HEAD_DIM = 128
ATTN_HEAD_DIM = 64
ATTN_GROUP = 4
WINDOW = 128
N_EXPERTS = 32
TOP_K = 4
SWIGLU_LIMIT = 7.0
SWIGLU_ALPHA = 1.702
NORM_EPS = 1e-5

LANES = 128
VMEM_LIMIT_BYTES = 56 * 1024 * 1024

INPROJ_TILE = 256
SCAN_TILE = 128
ATTN_TILE = 256
DISPATCH_TILE = 256
COMBINE_TILE = 128
EXPERT_BLOCK = 256


def _sigmoid(x):
    return 1.0 / (1.0 + jnp.exp(-x))


def _silu(x):
    return x * _sigmoid(x)


def _split_bf16(x):
    hi = x.astype(BF16)
    lo = (x - hi.astype(F32)).astype(BF16)
    return hi, lo


def _dot_f32ish(a, b, dims):
    ah, al = _split_bf16(a)
    bh, bl = _split_bf16(b)
    dn = (dims, ((), ()))
    out = lax.dot_general(ah, bh, dn, preferred_element_type=F32)
    out = out + lax.dot_general(ah, bl, dn, preferred_element_type=F32)
    out = out + lax.dot_general(al, bh, dn, preferred_element_type=F32)
    return out


_NN = ((1,), (0,))
_NT = ((1,), (1,))
_TN = ((0,), (0,))


def _dot(a, b, dims=_NN):
    return lax.dot_general(a, b, (dims, ((), ())), preferred_element_type=F32)


def _params(*sem):
    return pltpu.CompilerParams(dimension_semantics=sem, vmem_limit_bytes=VMEM_LIMIT_BYTES)


def _ada_kernel(c_ref, w_ref, b_ref, o_ref):
    ca = _silu(c_ref[...])
    o_ref[...] = _dot_f32ish(ca, w_ref[...], _NN) + b_ref[...]


def _ada_linear(c, w3, b2, layer):
    bsz, d = c.shape
    m = w3.shape[2]
    tn = 1024
    return pl.pallas_call(
        _ada_kernel,
        grid=(m // tn,),
        in_specs=[
            pl.BlockSpec((bsz, d), lambda j: (0, 0)),
            pl.BlockSpec((None, d, tn), lambda j: (layer, 0, j)),
            pl.BlockSpec((None, 1, tn), lambda j: (layer, 0, j)),
        ],
        out_specs=pl.BlockSpec((bsz, tn), lambda j: (0, j)),
        out_shape=jax.ShapeDtypeStruct((bsz, m), F32),
        compiler_params=_params("arbitrary"),
        name="ada_linear",
    )(c, w3, b2.reshape(b2.shape[0], 1, m))


def _pack_rows(h):
    d = h.shape[1]
    bits = lax.bitcast_convert_type(h.astype(BF16).astype(F32), U32)
    return bits[:, : d // 2] | (bits[:, d // 2:] >> 16)


def _unpack_rows(p):
    a = lax.bitcast_convert_type(p & jnp.uint32(0xFFFF0000), F32)
    b = lax.bitcast_convert_type(p << 16, F32)
    return jnp.concatenate([a, b], axis=1).astype(BF16)


def _ffn_router_tail(x1, fnw_ref, shf_ref, scf_ref, wrt_ref, br_ref, cnt_ref,
                     hp_ref, idx_ref, gate_ref, rank_ref, counts_ref, *, first_step):
    t, d = x1.shape
    ne = wrt_ref.shape[0]
    r = lax.rsqrt(jnp.mean(x1 * x1, axis=-1, keepdims=True) + NORM_EPS)
    h2 = (x1 * r * fnw_ref[...]) * (1.0 + scf_ref[...]) + shf_ref[...]
    hp_ref[...] = _pack_rows(h2)

    logits = _dot_f32ish(wrt_ref[...], h2, _NT) + br_ref[...]
    e_iota = lax.broadcasted_iota(I32, (ne, t), 0).astype(F32)
    work = logits
    sels, vals, idxs = [], [], []
    for _ in range(TOP_K):
        m = jnp.max(work, axis=0, keepdims=True)
        idx = jnp.min(jnp.where(work == m, e_iota, float(ne)), axis=0, keepdims=True)
        sel = e_iota == idx
        work = jnp.where(sel, -jnp.inf, work)
        sels.append(sel)
        vals.append(m)
        idxs.append(idx)
    exps = [jnp.exp(v - vals[0]) for v in vals]
    denom = exps[0] + exps[1] + exps[2] + exps[3]

    @pl.when(first_step)
    def _():
        cnt_ref[...] = jnp.zeros_like(cnt_ref)

    onehot = jnp.zeros((ne, t), F32)
    for sel in sels:
        onehot = onehot + sel.astype(F32)
    upper = (lax.broadcasted_iota(I32, (t, t), 0) < lax.broadcasted_iota(I32, (t, t), 1))
    cum = _dot(onehot.astype(BF16), upper.astype(F32).astype(BF16))
    cnt = cnt_ref[...]
    cum = cum + jnp.concatenate([cnt] * (t // LANES), axis=1)
    for k in range(TOP_K):
        rank = jnp.sum(jnp.where(sels[k], cum, 0.0), axis=0, keepdims=True)
        idx_ref[k:k + 1, :] = idxs[k].astype(I32)
        gate_ref[k:k + 1, :] = exps[k] / denom
        rank_ref[k:k + 1, :] = rank.astype(I32)
    new_cnt = cnt + jnp.sum(onehot, axis=1, keepdims=True)
    cnt_ref[...] = new_cnt
    counts_ref[...] = new_cnt


def _tail_in_specs(d, ne, bidx):
    return [
        pl.BlockSpec((1, d), lambda *g: (0, 0)),
        pl.BlockSpec((None, 1, d), lambda *g: (bidx(*g), 0, 0)),
        pl.BlockSpec((None, 1, d), lambda *g: (bidx(*g), 0, 0)),
        pl.BlockSpec((ne, d), lambda *g: (0, 0)),
        pl.BlockSpec((ne, 1), lambda *g: (0, 0)),
    ]


def _tail_out(n, d, ne, t, tidx):
    specs = [
        pl.BlockSpec((t, d // 2), lambda *g: (tidx(*g), 0)),
        pl.BlockSpec((TOP_K, t), lambda *g: (0, tidx(*g))),
        pl.BlockSpec((TOP_K, t), lambda *g: (0, tidx(*g))),
        pl.BlockSpec((TOP_K, t), lambda *g: (0, tidx(*g))),
        pl.BlockSpec((ne, LANES), lambda *g: (0, 0)),
    ]
    shapes = [
        jax.ShapeDtypeStruct((n, d // 2), U32),
        jax.ShapeDtypeStruct((TOP_K, n), I32),
        jax.ShapeDtypeStruct((TOP_K, n), F32),
        jax.ShapeDtypeStruct((TOP_K, n), I32),
        jax.ShapeDtypeStruct((ne, LANES), F32),
    ]
    return specs, shapes


def _hgrn_inproj_kernel(x_ref, nw_ref, sh_ref, sc_ref, win_ref, lbl_ref,
                        q_ref, k_ref, v_ref, lf_ref, g_ref):
    x = x_ref[...]
    d = x.shape[1]
    r = lax.rsqrt(jnp.mean(x * x, axis=-1, keepdims=True) + NORM_EPS)
    h = (x * r * nw_ref[...]) * (1.0 + sc_ref[...]) + sh_ref[...]
    proj = _dot(h.astype(BF16), win_ref[...])
    lbl = lbl_ref[...]
    e = jnp.exp(lbl - jnp.max(lbl, axis=0, keepdims=True))
    lb = e[0:1, :] / jnp.sum(e, axis=0, keepdims=True)
    f = lb + (1.0 - lb) * _sigmoid(proj[:, d:2 * d])
    q_ref[...] = _silu(proj[:, :d]).astype(BF16)
    k_ref[...] = (1.0 - f).astype(BF16)
    v_ref[...] = proj[:, 2 * d:3 * d].astype(BF16)
    lf_ref[...] = jnp.log(f)
    g_ref[...] = _silu(proj[:, 3 * d:]).astype(BF16)


def _hgrn_inproj(x2, nw, sh, sc, w_in_b, lbl, seq):
    n, d = x2.shape
    t = INPROJ_TILE
    per_b = seq // t
    row = pl.BlockSpec((t, d), lambda i: (i, 0))
    vec = pl.BlockSpec((None, 1, d), lambda i: (i // per_b, 0, 0))
    return pl.pallas_call(
        _hgrn_inproj_kernel,
        grid=(n // t,),
        in_specs=[row, pl.BlockSpec((1, d), lambda i: (0, 0)), vec, vec,
                  pl.BlockSpec((d, 4 * d), lambda i: (0, 0)),
                  pl.BlockSpec(lbl.shape, lambda i: (0, 0))],
        out_specs=[row] * 5,
        out_shape=[jax.ShapeDtypeStruct((n, d), BF16)] * 3
        + [jax.ShapeDtypeStruct((n, d), F32), jax.ShapeDtypeStruct((n, d), BF16)],
        compiler_params=_params("arbitrary"),
        name="hgrn_inproj",
    )(x2, nw, sh, sc, w_in_b, lbl)


def _hgrn_scan_kernel(q_ref, k_ref, v_ref, lf_ref, g_ref, x_ref, gt_ref, hnw_ref, wout_ref,
                      fnw_ref, shf_ref, scf_ref, wrt_ref, br_ref,
                      x1_ref, hp_ref, idx_ref, gate_ref, rank_ref, counts_ref,
                      state_ref, cnt_ref):
    t, d = x_ref.shape
    nh = d // HGRN_HEAD_DIM
    hd = HGRN_HEAD_DIM
    first_tile = pl.program_id(1) == 0

    @pl.when(first_tile)
    def _():
        state_ref[...] = jnp.zeros_like(state_ref)

    q = q_ref[...].astype(F32)
    k = k_ref[...].astype(F32)
    v = v_ref[...]
    rows = lax.broadcasted_iota(I32, (t, d), 0)
    ti = lax.broadcasted_iota(I32, (t, t), 0)
    si = lax.broadcasted_iota(I32, (t, t), 1)

    scores = [jnp.where(ti == si, _dot(q[:, h * hd:(h + 1) * hd].astype(BF16),
                                       k[:, h * hd:(h + 1) * hd].astype(BF16), _NT), 0.0)
              for h in range(nh)]
    p = lf_ref[...]
    tot = p
    b = 1
    while b < t:
        qs = (q * jnp.exp(p)).astype(BF16)
        ks = (k * jnp.exp(tot - p)).astype(BF16)
        sh = (2 * b).bit_length() - 1
        mask = ((ti >> sh) == (si >> sh)) & ((ti & b) != 0) & ((si & b) == 0)
        for h in range(nh):
            sl = slice(h * hd, (h + 1) * hd)
            scores[h] = scores[h] + jnp.where(mask, _dot(qs[:, sl], ks[:, sl], _NT), 0.0)
        second = (rows & b) != 0
        prev_tot = pltpu.roll(tot, b, axis=0)
        next_tot = pltpu.roll(tot, t - b, axis=0)
        p = p + jnp.where(second, prev_tot, 0.0)
        tot = tot + jnp.where(second, prev_tot, next_tot)
        b *= 2
    qg = (q * jnp.exp(p)).astype(BF16)
    kg = (k * jnp.exp(tot - p)).astype(BF16)
    decay_end = jnp.exp(tot[0:1, :])

    outs = []
    for h in range(nh):
        sl = slice(h * hd, (h + 1) * hd)
        st = state_ref[h]
        o = _dot(scores[h].astype(BF16), v[:, sl]) + _dot(qg[:, sl], st.astype(BF16), _NT)
        state_ref[h] = st * decay_end[:, sl] + _dot(v[:, sl], kg[:, sl], _TN)
        rs = lax.rsqrt(jnp.mean(o * o, axis=-1, keepdims=True) + NORM_EPS)
        outs.append(o * rs)
    o_all = jnp.concatenate(outs, axis=1) * hnw_ref[...]
    z = (o_all * g_ref[...].astype(F32)).astype(BF16)
    y = _dot(z, wout_ref[...])
    x1 = x_ref[...] + gt_ref[...] * y
    x1_ref[...] = x1

    first_step = jnp.logical_and(pl.program_id(0) == 0, first_tile)
    _ffn_router_tail(x1, fnw_ref, shf_ref, scf_ref, wrt_ref, br_ref, cnt_ref,
                     hp_ref, idx_ref, gate_ref, rank_ref, counts_ref, first_step=first_step)


def _hgrn_scan(qs, ks, vs, lf, gs, x2, gt, hnw_full, w_out_b, tail_args, bsz, seq):
    n, d = x2.shape
    t = SCAN_TILE
    per_b = seq // t
    ne = tail_args[3].shape[0]
    tidx = lambda b, j: b * per_b + j
    row = pl.BlockSpec((t, d), lambda b, j: (tidx(b, j), 0))
    vec = pl.BlockSpec((None, 1, d), lambda b, j: (b, 0, 0))
    tail_specs, tail_shapes = _tail_out(n, d, ne, t, tidx)
    return pl.pallas_call(
        _hgrn_scan_kernel,
        grid=(bsz, per_b),
        in_specs=[row] * 6 + [vec, pl.BlockSpec((1, d), lambda b, j: (0, 0)),
                              pl.BlockSpec((d, d), lambda b, j: (0, 0))]
        + _tail_in_specs(d, ne, lambda b, j: b),
        out_specs=[row] + tail_specs,
        out_shape=[jax.ShapeDtypeStruct((n, d), F32)] + tail_shapes,
        scratch_shapes=[pltpu.VMEM((d // HGRN_HEAD_DIM, HGRN_HEAD_DIM, HGRN_HEAD_DIM), F32),
                        pltpu.VMEM((ne, LANES), F32)],
        compiler_params=_params("arbitrary", "arbitrary"),
        name="hgrn_scan",
    )(qs, ks, vs, lf, gs, x2, gt, hnw_full, w_out_b, *tail_args)


def _attn_kernel(x_ref, mnw_ref, shm_ref, scm_ref, knw_ref, shk_ref, sck_ref, wq_ref, wkv_ref,
                 sink_ref, slope_ref, wout_ref, gt_ref,
                 fnw_ref, shf_ref, scf_ref, wrt_ref, br_ref,
                 x2_ref, hp_ref, idx_ref, gate_ref, rank_ref, counts_ref,
                 kprev_ref, vprev_ref, cnt_ref):
    t, d = x_ref.shape
    blk = WINDOW
    dh = ATTN_HEAD_DIM
    n_heads = d // dh
    n_pairs = n_heads // 2
    kvw = wkv_ref.shape[1] // 2
    first_tile = pl.program_id(1) == 0

    @pl.when(first_tile)
    def _():
        kprev_ref[...] = jnp.zeros_like(kprev_ref)
        vprev_ref[...] = jnp.zeros_like(vprev_ref)

    x = x_ref[...]
    r = lax.rsqrt(jnp.mean(x * x, axis=-1, keepdims=True) + NORM_EPS)
    xn = x * r
    hq = (xn * mnw_ref[...]) * (1.0 + scm_ref[...]) + shm_ref[...]
    hkv = (xn * knw_ref[...]) * (1.0 + sck_ref[...]) + shk_ref[...]
    q = _dot(hq.astype(BF16), wq_ref[...]) * (dh ** -0.5)
    kv = _dot(hkv.astype(BF16), wkv_ref[...])
    k_all = jnp.concatenate([kprev_ref[...], kv[:, :kvw]], axis=0)
    v_all = jnp.concatenate([vprev_ref[...], kv[:, kvw:]], axis=0)
    kprev_ref[...] = kv[t - blk:, :kvw]
    vprev_ref[...] = kv[t - blk:, kvw:]

    qi = lax.broadcasted_iota(I32, (blk, 2 * blk), 0)
    ki = lax.broadcasted_iota(I32, (blk, 2 * blk), 1)
    dist = blk + qi - ki
    in_window = (dist >= 0) & (dist < WINDOW)
    dist_f = dist.astype(F32)
    lane = lax.broadcasted_iota(I32, (2 * blk, LANES), 1)
    low_half = lane < dh

    out_blocks = []
    for j in range(t // blk):
        if j == 0:
            valid = in_window & (ki >= jnp.where(first_tile, blk, 0))
        else:
            valid = in_window
        qb = q[j * blk:(j + 1) * blk, :].astype(BF16)
        pair_outs = []
        for pr in range(n_pairs):
            hk = (2 * pr) // ATTN_GROUP
            c = hk // 2
            kt = k_all[j * blk:(j + 2) * blk, c * LANES:(c + 1) * LANES]
            vt = v_all[j * blk:(j + 2) * blk, c * LANES:(c + 1) * LANES]
            if hk % 2 == 1:
                kt = pltpu.roll(kt, dh, axis=1)
                vt = pltpu.roll(vt, dh, axis=1)
            kt_sw = pltpu.roll(kt, dh, axis=1)
            vt_sw = pltpu.roll(vt, dh, axis=1)
            q2 = qb[:, pr * LANES:(pr + 1) * LANES]
            acc = None
            for half in range(2):
                head = 2 * pr + half
                if half == 0:
                    kk = jnp.where(low_half, kt, 0.0).astype(BF16)
                    vv = jnp.where(low_half, vt, 0.0).astype(BF16)
                else:
                    kk = jnp.where(low_half, 0.0, kt_sw).astype(BF16)
                    vv = jnp.where(low_half, 0.0, vt_sw).astype(BF16)
                s = _dot(q2, kk, _NT) - slope_ref[head] * dist_f
                s = jnp.where(valid, s, -jnp.inf)
                sink = sink_ref[head]
                m = jnp.maximum(jnp.max(s, axis=-1, keepdims=True), sink)
                pexp = jnp.exp(s - m)
                den = jnp.sum(pexp, axis=-1, keepdims=True) + jnp.exp(sink - m)
                probs = (pexp / den).astype(BF16)
                o = _dot(probs, vv)
                acc = o if acc is None else acc + o
            pair_outs.append(acc)
        out_blocks.append(jnp.concatenate(pair_outs, axis=1))
    o_all = jnp.concatenate(out_blocks, axis=0).astype(BF16)
    y = _dot(o_all, wout_ref[...])
    x2 = x + gt_ref[...] * y
    x2_ref[...] = x2

    first_step = jnp.logical_and(pl.program_id(0) == 0, first_tile)
    _ffn_router_tail(x2, fnw_ref, shf_ref, scf_ref, wrt_ref, br_ref, cnt_ref,
                     hp_ref, idx_ref, gate_ref, rank_ref, counts_ref, first_step=first_step)


def _attn_mixer(x2d, mnw, shm, scm, knw, shk, sck, wq_b, wkv_b, sinks, slopes, wout_b, gt,
                tail_args, bsz, seq):
    n, d = x2d.shape
    t = ATTN_TILE
    per_b = seq // t
    ne = tail_args[3].shape[0]
    kvw = wkv_b.shape[1] // 2
    tidx = lambda b, j: b * per_b + j
    row = pl.BlockSpec((t, d), lambda b, j: (tidx(b, j), 0))
    vec = pl.BlockSpec((None, 1, d), lambda b, j: (b, 0, 0))
    const = lambda shape: pl.BlockSpec(shape, lambda b, j: (0,) * len(shape))
    smem = pl.BlockSpec(memory_space=pltpu.SMEM)
    tail_specs, tail_shapes = _tail_out(n, d, ne, t, tidx)
    return pl.pallas_call(
        _attn_kernel,
        grid=(bsz, per_b),
        in_specs=[row, const((1, d)), vec, vec, const((1, d)), vec, vec,
                  const((d, d)), const((d, 2 * kvw)), smem, smem, const((d, d)), vec]
        + _tail_in_specs(d, ne, lambda b, j: b),
        out_specs=[row] + tail_specs,
        out_shape=[jax.ShapeDtypeStruct((n, d), F32)] + tail_shapes,
        scratch_shapes=[pltpu.VMEM((WINDOW, kvw), F32), pltpu.VMEM((WINDOW, kvw), F32),
                        pltpu.VMEM((ne, LANES), F32)],
        compiler_params=_params("arbitrary", "arbitrary"),
        name="attn_mixer",
    )(x2d, mnw, shm, scm, knw, shk, sck, wq_b, wkv_b, sinks, slopes, wout_b, gt, *tail_args)


def _dispatch_kernel(idx_ref, rank_ref, start_ref, cnt_ref, na_ref, hp_ref, xs_ref, dest_ref,
                     zero_ref, sem, zsem, *, ne, bm, n_blocks):
    t = hp_ref.shape[0]

    def issue(tok, carry):
        for k in range(TOP_K):
            dst = start_ref[idx_ref[k, tok]] + rank_ref[k, tok]
            dest_ref[k, tok] = dst
            pltpu.make_async_copy(hp_ref.at[pl.ds(tok, 1)], xs_ref.at[pl.ds(dst, 1)], sem).start()
        return carry

    lax.fori_loop(0, t, issue, 0)

    @pl.when(pl.program_id(0) == pl.num_programs(0) - 1)
    def _():
        zero_ref[...] = jnp.zeros_like(zero_ref)

        def tail_copy(e, rw):
            return pltpu.make_async_copy(zero_ref.at[pl.ds(0, 1)],
                                         xs_ref.at[pl.ds(start_ref[e] + rw, 1)], zsem)

        def per_expert(e, carry):
            c = cnt_ref[e]
            end = ((c + bm - 1) // bm) * bm
            lax.fori_loop(c, end, lambda rw, inner: (tail_copy(e, rw).start(), inner)[1], 0)
            lax.fori_loop(c, end, lambda rw, inner: (tail_copy(e, rw).wait(), inner)[1], 0)
            return carry

        lax.fori_loop(0, ne, per_expert, 0)

        def block_copy(g):
            return pltpu.make_async_copy(zero_ref, xs_ref.at[pl.ds(pl.multiple_of(g * bm, bm), bm)], zsem)

        lax.fori_loop(na_ref[0], n_blocks, lambda g, inner: (block_copy(g).start(), inner)[1], 0)
        lax.fori_loop(na_ref[0], n_blocks, lambda g, inner: (block_copy(g).wait(), inner)[1], 0)

    for _ in range(TOP_K):
        pltpu.make_async_copy(hp_ref, xs_ref.at[pl.ds(0, t)], sem).wait()


def _dispatch(idx, rank, starts, counts_i, n_active, hp, ne, bm, n_blocks):
    n, dp = hp.shape
    t = DISPATCH_TILE
    tok_smem = pl.BlockSpec((TOP_K, t), lambda i: (0, i), memory_space=pltpu.SMEM)
    smem = pl.BlockSpec(memory_space=pltpu.SMEM)
    return pl.pallas_call(
        functools.partial(_dispatch_kernel, ne=ne, bm=bm, n_blocks=n_blocks),
        grid=(n // t,),
        in_specs=[tok_smem, tok_smem, smem, smem, smem, pl.BlockSpec((t, dp), lambda i: (i, 0))],
        out_specs=[pl.BlockSpec(memory_space=pl.ANY), tok_smem],
        out_shape=[jax.ShapeDtypeStruct((n_blocks * bm, dp), U32),
                   jax.ShapeDtypeStruct((TOP_K, n), I32)],
        scratch_shapes=[pltpu.VMEM((bm, dp), U32), pltpu.SemaphoreType.DMA, pltpu.SemaphoreType.DMA],
        compiler_params=_params("arbitrary"),
        name="moe_dispatch",
    )(idx, rank, starts, counts_i, n_active, hp)


def _expert_kernel(be_ref, na_ref, xs_ref, w1_ref, b1_ref, w2_ref, b2_ref, y_ref,
                   w1b_ref, w2b_ref):
    g = pl.program_id(0)
    dff = w2_ref.shape[0]
    new_expert = jnp.logical_or(g == 0, be_ref[g] != be_ref[jnp.maximum(g - 1, 0)])
    chunk = 128

    @pl.when(new_expert)
    def _():
        def cast1(i, c):
            rws = pl.ds(pl.multiple_of(i * chunk, chunk), chunk)
            w1b_ref[rws, :] = w1_ref[rws, :].astype(BF16)
            return c

        def cast2(i, c):
            rws = pl.ds(pl.multiple_of(i * chunk, chunk), chunk)
            w2b_ref[rws, :] = w2_ref[rws, :].astype(BF16)
            return c

        lax.fori_loop(0, w1_ref.shape[0] // chunk, cast1, 0)
        lax.fori_loop(0, w2_ref.shape[0] // chunk, cast2, 0)

    @pl.when(g < na_ref[0])
    def _():
        x = _unpack_rows(xs_ref[...])
        u = _dot(x, w1b_ref[...]) + b1_ref[...]
        glu = jnp.minimum(u[:, :dff], SWIGLU_LIMIT)
        lin = jnp.clip(u[:, dff:], -SWIGLU_LIMIT, SWIGLU_LIMIT)
        act = glu * _sigmoid(SWIGLU_ALPHA * glu) * (lin + 1.0)
        y_ref[...] = _dot(act.astype(BF16), w2b_ref[...]) + b2_ref[...]

    @pl.when(g >= na_ref[0])
    def _():
        y_ref[...] = jnp.zeros_like(y_ref)


def _experts(xs, blk_exp, n_active, w1, b1, w2, b2, layer, n_blocks, bm):
    rows, dp = xs.shape
    _, ne, d, dff2 = w1.shape
    dff = w2.shape[2]
    grid_spec = pltpu.PrefetchScalarGridSpec(
        num_scalar_prefetch=2,
        grid=(n_blocks,),
        in_specs=[
            pl.BlockSpec((bm, dp), lambda g, be, na: (g, 0)),
            pl.BlockSpec((None, None, d, dff2), lambda g, be, na: (layer, be[g], 0, 0)),
            pl.BlockSpec((None, None, 1, dff2), lambda g, be, na: (layer, be[g], 0, 0)),
            pl.BlockSpec((None, None, dff, d), lambda g, be, na: (layer, be[g], 0, 0)),
            pl.BlockSpec((None, None, 1, d), lambda g, be, na: (layer, be[g], 0, 0)),
        ],
        out_specs=pl.BlockSpec((bm, d), lambda g, be, na: (g, 0)),
        scratch_shapes=[pltpu.VMEM((d, dff2), BF16), pltpu.VMEM((dff, d), BF16)],
    )
    return pl.pallas_call(
        _expert_kernel,
        grid_spec=grid_spec,
        out_shape=jax.ShapeDtypeStruct((rows, d), F32),
        compiler_params=_params("arbitrary"),
        name="moe_experts",
    )(blk_exp, n_active, xs, w1, b1.reshape(b1.shape[0], ne, 1, dff2), w2,
      b2.reshape(b2.shape[0], ne, 1, d))


def _combine_kernel(dest_ref, gate_ref, x_ref, gt_ref, fw_ref, ys_ref, o_ref, buf_ref, sem, *, final_norm):
    t, d = x_ref.shape

    def issue(tok, carry):
        for k in range(TOP_K):
            pltpu.make_async_copy(ys_ref.at[pl.ds(dest_ref[k, tok], 1)],
                                  buf_ref.at[k, pl.ds(tok, 1)], sem).start()
        return carry

    lax.fori_loop(0, t, issue, 0)
    for k in range(TOP_K):
        pltpu.make_async_copy(ys_ref.at[pl.ds(0, t)], buf_ref.at[k], sem).wait()

    gates = gate_ref[...]
    acc = gates[:, 0:1] * buf_ref[0]
    for k in range(1, TOP_K):
        acc = acc + gates[:, k:k + 1] * buf_ref[k]
    out = x_ref[...] + gt_ref[...] * acc
    if final_norm:
        out = out * lax.rsqrt(jnp.mean(out * out, axis=-1, keepdims=True) + NORM_EPS) * fw_ref[...]
    o_ref[...] = out


def _combine(dest, gates_nk, x2d, gt, fw, ys, seq, final_norm):
    n, d = x2d.shape
    t = COMBINE_TILE
    per_b = seq // t
    return pl.pallas_call(
        functools.partial(_combine_kernel, final_norm=final_norm),
        grid=(n // t,),
        in_specs=[pl.BlockSpec((TOP_K, t), lambda i: (0, i), memory_space=pltpu.SMEM),
                  pl.BlockSpec((t, TOP_K), lambda i: (i, 0)),
                  pl.BlockSpec((t, d), lambda i: (i, 0)),
                  pl.BlockSpec((None, 1, d), lambda i: (i // per_b, 0, 0)),
                  pl.BlockSpec((1, d), lambda i: (0, 0)),
                  pl.BlockSpec(memory_space=pl.ANY)],
        out_specs=pl.BlockSpec((t, d), lambda i: (i, 0)),
        out_shape=jax.ShapeDtypeStruct((n, d), F32),
        scratch_shapes=[pltpu.VMEM((TOP_K, t, d), F32), pltpu.SemaphoreType.DMA],
        compiler_params=_params("arbitrary"),
        name="moe_combine",
    )(dest, gates_nk, x2d, gt, fw, ys)


def _moe_layer(x_res, hp, idx, gate_t, rank, counts, gt_f, w1, b1, w2, b2, layer, seq, final_w):
    n, d = x_res.shape
    ne = w1.shape[1]
    bm = EXPERT_BLOCK
    n_blocks = (n * TOP_K) // bm + ne
    cnt = counts[:, 0].astype(I32)
    nblk = (cnt + bm - 1) // bm
    ends = jnp.cumsum(nblk)
    starts = ends - nblk
    n_active = ends[-1:]
    g = jnp.minimum(jnp.arange(n_blocks, dtype=I32), n_active - 1)
    blk_exp = jnp.minimum(jnp.searchsorted(ends, g, side="right"), ne - 1).astype(I32)
    xs, dest = _dispatch(idx, rank, starts * bm, cnt, n_active, hp, ne, bm, n_blocks)
    ys = _experts(xs, blk_exp, n_active, w1, b1, w2, b2, layer, n_blocks, bm)
    fw = jnp.ones((1, d), F32) if final_w is None else final_w.reshape(1, d)
    return _combine(dest, gate_t.T, x_res, gt_f, fw, ys, seq, final_w is not None)


def kernel(x, c, ada_w, ada_b, mix_norm_w, ffn_norm_w, hgrn_w_in, hgrn_lb_logits, hgrn_norm_w,
           hgrn_w_out, kv_norm_w, kv_ada_w, kv_ada_b, w_kv, attn_w_q, attn_sinks, attn_w_out,
           router_w, router_b, moe_w1, moe_b1, moe_w2, moe_b2, final_norm_w):
    bsz, seq, d = x.shape
    n = bsz * seq
    depth = ada_w.shape[0]
    n_a = hgrn_w_in.shape[0]
    assert depth == 2 and n_a == 1, "kernel is written for one HGRN2 layer followed by one attention layer"
    ne = router_w.shape[2]
    x2d = x.reshape(n, d)

    def mods(layer):
        a = _ada_linear(c, ada_w, ada_b, layer)
        return [a[:, i * d:(i + 1) * d].reshape(bsz, 1, d) for i in range(6)]

    def tail_args(layer, sh_f, sc_f):
        return (ffn_norm_w[layer].reshape(1, d), sh_f, sc_f,
                router_w[layer].T, router_b[layer].reshape(ne, 1))

    sh_m, sc_m, gt_m, sh_f, sc_f, gt_f = mods(0)
    qs, ks, vs, lf, gs = _hgrn_inproj(x2d, mix_norm_w[0].reshape(1, d), sh_m, sc_m,
                                      hgrn_w_in[0].astype(BF16), hgrn_lb_logits, seq)
    hnw_full = jnp.tile(hgrn_norm_w[0], d // HGRN_HEAD_DIM).reshape(1, d)
    x1, hp, idx, gate_t, rank, counts = _hgrn_scan(
        qs, ks, vs, lf, gs, x2d, gt_m, hnw_full, hgrn_w_out[0].astype(BF16),
        tail_args(0, sh_f, sc_f), bsz, seq)
    x1 = _moe_layer(x1, hp, idx, gate_t, rank, counts, gt_f, moe_w1, moe_b1, moe_w2, moe_b2, 0, seq,
                    None)

    kv_mod = _ada_linear(c, kv_ada_w.reshape(1, d, 2 * d), kv_ada_b.reshape(1, 2 * d), 0)
    kv_shift = kv_mod[:, :d].reshape(bsz, 1, d)
    kv_scale = kv_mod[:, d:].reshape(bsz, 1, d)
    sh_m, sc_m, gt_m, sh_f, sc_f, gt_f = mods(1)
    n_heads = d // ATTN_HEAD_DIM
    slopes = jnp.exp2(-8.0 * jnp.arange(1, n_heads + 1, dtype=F32) / n_heads)
    x2, hp, idx, gate_t, rank, counts = _attn_mixer(
        x1, mix_norm_w[1].reshape(1, d), sh_m, sc_m, kv_norm_w.reshape(1, d), kv_shift, kv_scale,
        attn_w_q[0].astype(BF16), w_kv.astype(BF16), attn_sinks[0].astype(F32), slopes,
        attn_w_out[0].astype(BF16), gt_m, tail_args(1, sh_f, sc_f), bsz, seq)
    out = _moe_layer(x2, hp, idx, gate_t, rank, counts, gt_f, moe_w1, moe_b1, moe_w2, moe_b2, 1, seq,
                     final_norm_w)
    return out.reshape(bsz, seq, d)
```
